```python
import math
import jax, jax.numpy as jnp
from jax import lax
import numpy as np

D_MODEL = 1024
BATCH = 2
SEQ = 8192
DEPTH = 2

GRID_W = 64
CTX_LEN = 256
D_FF = 2816
N_MOD = 9
LN_EPS = 1e-5
S5_WIDTH = 512
S5_GROUP = 16
S5_GROUPS = S5_WIDTH // S5_GROUP
S5_STATE = 64
S5_DT_MIN = 0.001
S5_DT_MAX = 0.1
NA_HEADS = 8
NA_HEAD_DIM = 64
NA_WIDTH = NA_HEADS * NA_HEAD_DIM
WIN_H = 8
WIN_W = 16
CONV_WIDTH = 512
CONV_K = 3
N_BRANCH = 3
BRANCH_WIDTH = 512
COL_U = 0
COL_K = COL_U + S5_WIDTH
COL_V = COL_K + NA_WIDTH
COL_Q = COL_V + NA_WIDTH
COL_Z = COL_Q + NA_WIDTH
COL_B = COL_Z + CONV_WIDTH
COL_C = COL_B + CONV_WIDTH
COL_G = COL_C + CONV_WIDTH
D_IN = COL_G + N_BRANCH * D_MODEL

kernel_name = 'hybrid_s5_natten_shortconv_macaron_trunk'


def layer_norm(x, g, b):
    xf = x.astype(jnp.float32)
    mu = jnp.mean(xf, -1, keepdims=True)
    var = jnp.mean(jnp.square(xf - mu), -1, keepdims=True)
    y = (xf - mu) * lax.rsqrt(var + LN_EPS) * g.astype(jnp.float32) + b.astype(jnp.float32)
    return y.astype(x.dtype)


def modulate(x, shift, scale):
    return x * (1.0 + scale) + shift


def swiglu(h, wg, wu, wd):
    return (jax.nn.silu(h @ wg) * (h @ wu)) @ wd


def ffn_sublayer(x, shift, scale, gate, wg, wu, wd, g, b, alpha):
    h = modulate(x, shift, scale)
    return layer_norm(alpha * x + 0.5 * gate * swiglu(h, wg, wu, wd), g, b)


def split_heads(t):
    return t.reshape(t.shape[:-1] + (NA_HEADS, NA_HEAD_DIM))


def s5_discretise(lam_re, lam_im, log_dt, b_re, b_im, c_re, c_im):
    f32 = jnp.float32
    lam = lax.complex(lam_re.astype(f32), lam_im.astype(f32))
    dt = jnp.exp(log_dt.astype(f32))[..., None]
    lam_bar = jnp.exp(lam * dt)
    b = lax.complex(b_re.astype(f32), b_im.astype(f32))
    b_bar = ((lam_bar - 1.0) / lam)[..., None] * b
    cc = lax.complex(c_re.astype(f32), c_im.astype(f32))
    return lam_bar, b_bar, cc


def _recurrence_combine(e1, e2):
    a1, b1 = e1
    a2, b2 = e2
    return a1 * a2, a2 * b1 + b2


def s5_scan(ug, lam_bar, b_bar, s0, reverse):
    t_len = ug.shape[1]
    if reverse:
        ug = jnp.flip(ug, 1)
    bu = jnp.einsum('gph,btgh->btgp', b_bar, ug.astype(jnp.float32).astype(jnp.complex64))
    bu = bu.at[:, 0].add(lam_bar * s0)
    a = jnp.broadcast_to(lam_bar, (1, t_len) + lam_bar.shape)
    _, states = lax.associative_scan(_recurrence_combine, (a, bu), axis=1)
    if reverse:
        states = jnp.flip(states, 1)
    return states


def s5_states(u, lam_bar, b_bar, s0_fwd, s0_bwd):
    bsz, t_len, _ = u.shape
    ug = u.reshape(bsz, t_len, S5_GROUPS, S5_GROUP)
    s_fwd = s5_scan(ug, lam_bar[0], b_bar[0], s0_fwd, False)
    s_bwd = s5_scan(ug, lam_bar[1], b_bar[1], s0_bwd, True)
    return s_fwd, s_bwd


def s5_readout(u, s_fwd, s_bwd, cc, d, w_glu):
    bsz, t_len, _ = u.shape
    y = jnp.real(jnp.einsum('ghp,btgp->btgh', cc[0], s_fwd) + jnp.einsum('ghp,btgp->btgh', cc[1], s_bwd))
    y = y.reshape(bsz, t_len, S5_WIDTH).astype(u.dtype) + d * u
    g = jax.nn.gelu(y)
    return g * jax.nn.sigmoid(g @ w_glu)


def neighbourhood_attention(q, k, v, k_ctx, v_ctx, rpb):
    bsz, s_len, n_h, dh = q.shape
    rows = s_len // GRID_W
    kh = min(WIN_H, rows)
    r = jnp.arange(rows)
    row_start = jnp.clip(r - WIN_H // 2, 0, rows - kh)
    row_idx = row_start[:, None] + jnp.arange(kh)[None, :]
    col = jnp.arange(GRID_W)
    col_start = jnp.clip(col - WIN_W // 2, 0, GRID_W - WIN_W)
    col_mask = (col[None, :] >= col_start[:, None]) & (col[None, :] < col_start[:, None] + WIN_W)
    dr = row_idx - r[:, None] + WIN_H - 1
    dc = jnp.clip(col[None, :] - col[:, None] + WIN_W - 1, 0, 2 * WIN_W - 2)
    bias = rpb.astype(jnp.float32)[:, dr]
    bias = jnp.transpose(bias[..., dc], (0, 1, 3, 2, 4))
    scale = dh ** -0.5
    qg = q.reshape(bsz, rows, GRID_W, n_h, dh)
    kg = k.reshape(bsz, rows, GRID_W, n_h, dh)[:, row_idx]
    vg = v.reshape(bsz, rows, GRID_W, n_h, dh)[:, row_idx].reshape(bsz, rows, kh * GRID_W, n_h, dh)
    s_win = jnp.einsum('brchd,brjkhd->bhrcjk', qg, kg).astype(jnp.float32) * scale + bias[None]
    s_win = jnp.where(col_mask[:, None, :], s_win, -jnp.inf).reshape(bsz, n_h, rows, GRID_W, kh * GRID_W)
    s_ctx = jnp.einsum('brchd,blhd->bhrcl', qg, k_ctx).astype(jnp.float32) * scale
    p = jax.nn.softmax(jnp.concatenate([s_win, s_ctx], axis=-1), axis=-1).astype(v.dtype)
    n_win = kh * GRID_W
    out = (jnp.einsum('bhrcn,brnhd->brchd', p[..., :n_win], vg)
           + jnp.einsum('bhrcl,blhd->brchd', p[..., n_win:], v_ctx))
    return out.reshape(bsz, s_len, n_h * dh)


def context_attention(q, k, v):
    bsz, l_len, n_h, dh = q.shape
    s = jnp.einsum('blhd,bmhd->bhlm', q, k).astype(jnp.float32) * dh ** -0.5
    p = jax.nn.softmax(s, axis=-1).astype(v.dtype)
    return jnp.einsum('bhlm,bmhd->blhd', p, v).reshape(bsz, l_len, n_h * dh)


def short_conv(z, bg, cg, conv_w):
    vv = cg * z
    t_len = vv.shape[1]
    pad = CONV_K // 2
    vp = jnp.pad(vv, ((0, 0), (pad, pad), (0, 0)))
    y = conv_w[0] * vp[:, 0:t_len]
    for i in range(1, CONV_K):
        y = y + conv_w[i] * vp[:, i:i + t_len]
    return bg * y


def merge_branches(gate_logits, y_a, y_b, y_c, w_branch, w_out):
    g = jax.nn.sigmoid(gate_logits).reshape(gate_logits.shape[:-1] + (N_BRANCH, D_MODEL))
    m = (g[..., 0, :] * (y_a @ w_branch[0]) + g[..., 1, :] * (y_b @ w_branch[1])
         + g[..., 2, :] * (y_c @ w_branch[2]))
    return m @ w_out


def setup_inputs(seed: int = 0) -> dict:
    key = jax.random.key(seed)
    ks = jax.random.split(key, 26)
    f32 = jnp.float32
    beta = (8.0 * DEPTH) ** -0.25

    def nrm(k, shape, s):
        return jax.random.normal(k, shape, f32) * s

    G, P, Hg = S5_GROUPS, S5_STATE, S5_GROUP
    n = jnp.arange(S5_STATE, dtype=f32)
    return {
        'x': nrm(ks[0], (BATCH, SEQ, D_MODEL), 1.0),
        'c': nrm(ks[1], (BATCH, D_MODEL), 1.0),
        'ctx': nrm(ks[2], (BATCH, CTX_LEN, D_MODEL), 1.0),
        'c_ctx': nrm(ks[3], (D_MODEL,), 1.0),
        'w_mod': nrm(ks[4], (DEPTH, D_MODEL, N_MOD * D_MODEL), 0.5 * D_MODEL ** -0.5),
        'b_mod': nrm(ks[5], (DEPTH, N_MOD * D_MODEL), 0.01),
        'ln_g': 1.0 + nrm(ks[6], (DEPTH, 3, D_MODEL), 0.02),
        'ln_b': nrm(ks[7], (DEPTH, 3, D_MODEL), 0.02),
        'ffn_wg': nrm(ks[8], (DEPTH, 2, D_MODEL, D_FF), D_MODEL ** -0.5),
        'ffn_wu': nrm(ks[9], (DEPTH, 2, D_MODEL, D_FF), D_MODEL ** -0.5),
        'ffn_wd': nrm(ks[10], (DEPTH, 2, D_FF, D_MODEL), beta * D_FF ** -0.5),
        'w_in': nrm(ks[11], (DEPTH, D_MODEL, D_IN), D_MODEL ** -0.5),
        's5_lam_re': -0.5 + nrm(ks[12], (DEPTH, 2, G, P), 0.01),
        's5_lam_im': math.pi * n + nrm(ks[13], (DEPTH, 2, G, P), 0.01),
        's5_log_dt': jax.random.uniform(ks[14], (DEPTH, 2, G), f32, math.log(S5_DT_MIN), math.log(S5_DT_MAX)),
        's5_b_re': nrm(ks[15], (DEPTH, 2, G, P, Hg), (2.0 * Hg) ** -0.5),
        's5_b_im': nrm(ks[16], (DEPTH, 2, G, P, Hg), (2.0 * Hg) ** -0.5),
        's5_c_re': nrm(ks[17], (DEPTH, 2, G, Hg, P), (2.0 * P) ** -0.5),
        's5_c_im': nrm(ks[18], (DEPTH, 2, G, Hg, P), (2.0 * P) ** -0.5),
        's5_d': nrm(ks[19], (DEPTH, S5_WIDTH), 1.0),
        's5_w_glu': nrm(ks[20], (DEPTH, S5_WIDTH, S5_WIDTH), S5_WIDTH ** -0.5),
        'na_rpb': nrm(ks[21], (DEPTH, NA_HEADS, 2 * WIN_H - 1, 2 * WIN_W - 1), 0.1),
        'conv_w': nrm(ks[22], (DEPTH, CONV_K, CONV_WIDTH), CONV_K ** -0.5),
        'w_branch': nrm(ks[23], (DEPTH, N_BRANCH, BRANCH_WIDTH, D_MODEL), BRANCH_WIDTH ** -0.5),
        'w_out': nrm(ks[24], (DEPTH, D_MODEL, D_MODEL), beta * D_MODEL ** -0.5),
    }


def reference(x, c, ctx, c_ctx, w_mod, b_mod, ln_g, ln_b, ffn_wg, ffn_wu, ffn_wd, w_in,
              s5_lam_re, s5_lam_im, s5_log_dt, s5_b_re, s5_b_im, s5_c_re, s5_c_im, s5_d, s5_w_glu,
              na_rpb, conv_w, w_branch, w_out):
    bsz = x.shape[0]
    alpha = (2.0 * DEPTH) ** 0.25
    silu_c = jax.nn.silu(c)[:, None, :]
    silu_cc = jax.nn.silu(c_ctx)
    xc = ctx
    for l in range(DEPTH):
        last = l == DEPTH - 1
        mod = (silu_c @ w_mod[l] + b_mod[l]).reshape(bsz, 1, N_MOD, D_MODEL)
        mod = [mod[:, :, i] for i in range(N_MOD)]
        modc = (silu_cc @ w_mod[l] + b_mod[l]).reshape(N_MOD, D_MODEL)

        x = ffn_sublayer(x, mod[0], mod[1], mod[2], ffn_wg[l, 0], ffn_wu[l, 0], ffn_wd[l, 0],
                         ln_g[l, 0], ln_b[l, 0], alpha)
        xc = ffn_sublayer(xc, modc[0], modc[1], modc[2], ffn_wg[l, 0], ffn_wu[l, 0], ffn_wd[l, 0],
                          ln_g[l, 0], ln_b[l, 0], alpha)

        lam_bar, b_bar, cc = s5_discretise(s5_lam_re[l], s5_lam_im[l], s5_log_dt[l],
                                           s5_b_re[l], s5_b_im[l], s5_c_re[l], s5_c_im[l])
        h = modulate(x, mod[3], mod[4])
        hc = modulate(xc, modc[3], modc[4])

        proj_c = hc @ (w_in[l][:, :COL_Q] if last else w_in[l])
        u_c = proj_c[..., COL_U:COL_K]
        k_c = split_heads(proj_c[..., COL_K:COL_V])
        v_c = split_heads(proj_c[..., COL_V:COL_Q])
        zero_state = jnp.zeros((bsz, S5_GROUPS, S5_STATE), jnp.complex64)
        sf_c, sb_c = s5_states(u_c, lam_bar, b_bar, zero_state, zero_state)

        proj = h @ w_in[l]
        u = proj[..., COL_U:COL_K]
        sf, sb = s5_states(u, lam_bar, b_bar, sf_c[:, -1], sb_c[:, 0])
        y_a = s5_readout(u, sf, sb, cc, s5_d[l], s5_w_glu[l])
        y_b = neighbourhood_attention(split_heads(proj[..., COL_Q:COL_Z]), split_heads(proj[..., COL_K:COL_V]),
                                      split_heads(proj[..., COL_V:COL_Q]), k_c, v_c, na_rpb[l])
        y_c = short_conv(proj[..., COL_Z:COL_B], proj[..., COL_B:COL_C], proj[..., COL_C:COL_G], conv_w[l])
        mix = merge_branches(proj[..., COL_G:], y_a, y_b, y_c, w_branch[l], w_out[l])
        x = layer_norm(alpha * x + mod[5] * mix, ln_g[l, 1], ln_b[l, 1])

        x = ffn_sublayer(x, mod[6], mod[7], mod[8], ffn_wg[l, 1], ffn_wu[l, 1], ffn_wd[l, 1],
                         ln_g[l, 2], ln_b[l, 2], alpha)

        if not last:
            yc_a = s5_readout(u_c, sf_c, sb_c, cc, s5_d[l], s5_w_glu[l])
            yc_b = context_attention(split_heads(proj_c[..., COL_Q:COL_Z]), k_c, v_c)
            yc_c = short_conv(proj_c[..., COL_Z:COL_B], proj_c[..., COL_B:COL_C], proj_c[..., COL_C:COL_G], conv_w[l])
            mix_c = merge_branches(proj_c[..., COL_G:], yc_a, yc_b, yc_c, w_branch[l], w_out[l])
            xc = layer_norm(alpha * xc + modc[5] * mix_c, ln_g[l, 1], ln_b[l, 1])
            xc = ffn_sublayer(xc, modc[6], modc[7], modc[8], ffn_wg[l, 1], ffn_wu[l, 1], ffn_wd[l, 1],
                              ln_g[l, 2], ln_b[l, 2], alpha)
    return x
```

```python
import functools
import math

import numpy as np
import jax
import jax.numpy as jnp
from jax import lax
from jax.experimental import pallas as pl
from jax.experimental.pallas import tpu as pltpu

F32 = jnp.float32
BF16 = jnp.bfloat16

D_MODEL = 1024
BATCH = 2
SEQ = 8192
DEPTH = 2
GRID_W = 64
GRID_ROWS = SEQ // GRID_W
CTX_LEN = 256
D_FF = 2816
N_MOD = 9
LN_EPS = 1e-5
S5_WIDTH = 512
S5_GROUP = 16
S5_GROUPS = S5_WIDTH // S5_GROUP
S5_STATE = 64
NA_HEADS = 8
NA_HEAD_DIM = 64
NA_WIDTH = NA_HEADS * NA_HEAD_DIM
WIN_H = 8
WIN_W = 16
CONV_WIDTH = 512
CONV_K = 3
COL_K = S5_WIDTH
COL_V = COL_K + NA_WIDTH
COL_Q = COL_V + NA_WIDTH
COL_Z = COL_Q + NA_WIDTH
COL_B = COL_Z + CONV_WIDTH
COL_C = COL_B + CONV_WIDTH
COL_G = COL_C + CONV_WIDTH
ALPHA = (2.0 * DEPTH) ** 0.25

R_LAT = BATCH * SEQ
R_CTX = BATCH * CTX_LEN
R_ALL = R_LAT + R_CTX
TM = 512
N_LAT_TILES = R_LAT // TM
N_ALL_TILES = R_ALL // TM
TILES_PER_BATCH = SEQ // TM

S5_CHUNK = 16
CHUNK_W = S5_CHUNK * S5_GROUP
N_CHUNKS = R_ALL // S5_CHUNK
LAT_CHUNKS = SEQ // S5_CHUNK
CTX_CHUNKS = CTX_LEN // S5_CHUNK
SCAN_ROWS = 8

ATT_QROWS = 4
ATT_KROWS = 12
ATT_STEPS = GRID_ROWS // ATT_QROWS
ATT_TQ = ATT_QROWS * GRID_W
ATT_TK = ATT_KROWS * GRID_W

VMEM_LIMIT = 56 * 1024 * 1024


def _const_spec(shape):
    nd = len(shape)
    return pl.BlockSpec(shape, lambda *_: (0,) * nd, pipeline_mode=pl.Buffered(1))


def _params(n_axes=1):
    return pltpu.CompilerParams(dimension_semantics=("arbitrary",) * n_axes,
                                vmem_limit_bytes=VMEM_LIMIT)


def _mod_row(t):
    return jnp.minimum(t // TILES_PER_BATCH, BATCH)


def _layer_norm(y, g, b):
    mu = jnp.mean(y, -1, keepdims=True)
    yc = y - mu
    var = jnp.mean(yc * yc, -1, keepdims=True)
    return yc * lax.rsqrt(var + LN_EPS) * g + b


def _dot(a, b):
    return jnp.dot(a, b, preferred_element_type=F32)


def _dot_t(a, b):
    return lax.dot_general(a, b, (((1,), (1,)), ((), ())), preferred_element_type=F32)


def _mod_kernel(c_ref, w_ref, b_ref, o_ref):
    cv = c_ref[...]
    s = (cv * jax.nn.sigmoid(cv)).astype(BF16)
    o_ref[0] = _dot(s, w_ref[0].astype(BF16)) + b_ref[0]


def _modulation(c, c_ctx, w_mod, b_mod):
    rows = 8
    cv = jnp.zeros((rows, D_MODEL), F32).at[:BATCH].set(c).at[BATCH].set(c_ctx)
    n_out = N_MOD * D_MODEL
    tn = 1152
    out = pl.pallas_call(
        _mod_kernel,
        out_shape=jax.ShapeDtypeStruct((DEPTH, rows, n_out), F32),
        grid=(DEPTH, n_out // tn),
        in_specs=[
            pl.BlockSpec((rows, D_MODEL), lambda l, j: (0, 0)),
            pl.BlockSpec((1, D_MODEL, tn), lambda l, j: (l, 0, j)),
            pl.BlockSpec((1, 1, tn), lambda l, j: (l, 0, j)),
        ],
        out_specs=pl.BlockSpec((1, rows, tn), lambda l, j: (l, 0, j)),
        compiler_params=_params(2),
        name="adaln_mod",
    )(cv, w_mod, b_mod.reshape(DEPTH, 1, n_out))
    return out[:, :BATCH + 1].reshape(DEPTH, BATCH + 1, N_MOD, D_MODEL)


def _ffn_kernel(x_ref, mod_ref, wg_ref, wu_ref, wd_ref, g_ref, b_ref, o_ref, *, k0):
    x = x_ref[...]
    shift = mod_ref[0, k0:k0 + 1, :]
    scale = mod_ref[0, k0 + 1:k0 + 2, :]
    gate = mod_ref[0, k0 + 2:k0 + 3, :]
    h = (x * (1.0 + scale) + shift).astype(BF16)
    a = _dot(h, wg_ref[...])
    u = _dot(h, wu_ref[...])
    t = (a * jax.nn.sigmoid(a) * u).astype(BF16)
    f = _dot(t, wd_ref[...])
    o_ref[...] = _layer_norm(ALPHA * x + (0.5 * gate) * f, g_ref[...], b_ref[...])


def _ffn(x, mod, wg, wu, wd, g, b, k0, n_tiles):
    return pl.pallas_call(
        functools.partial(_ffn_kernel, k0=k0),
        out_shape=jax.ShapeDtypeStruct((n_tiles * TM, D_MODEL), F32),
        grid=(n_tiles,),
        in_specs=[
            pl.BlockSpec((TM, D_MODEL), lambda t: (t, 0)),
            pl.BlockSpec((1, N_MOD, D_MODEL), lambda t: (_mod_row(t), 0, 0)),
            _const_spec((D_MODEL, D_FF)),
            _const_spec((D_MODEL, D_FF)),
            _const_spec((D_FF, D_MODEL)),
            _const_spec((1, D_MODEL)),
            _const_spec((1, D_MODEL)),
        ],
        out_specs=pl.BlockSpec((TM, D_MODEL), lambda t: (t, 0)),
        compiler_params=_params(),
        name="ffn_half_step",
    )(x, mod, wg, wu, wd, g, b)


def _proj_kernel(x_ref, mod_ref, w_ref, u_ref, k_ref, v_ref, q_ref):
    x = x_ref[...]
    shift = mod_ref[0, 3:4, :]
    scale = mod_ref[0, 4:5, :]
    h = (x * (1.0 + scale) + shift).astype(BF16)
    p = _dot(h, w_ref[...])
    u_ref[...] = p[:, 0:COL_K]
    k_ref[...] = p[:, COL_K:COL_V].astype(BF16)
    v_ref[...] = p[:, COL_V:COL_Q].astype(BF16)
    q_ref[...] = (p[:, COL_Q:COL_Z] * NA_HEAD_DIM ** -0.5).astype(BF16)


def _proj(x, mod, w_ukvq):
    row_spec = pl.BlockSpec((TM, S5_WIDTH), lambda t: (t, 0))
    return pl.pallas_call(
        _proj_kernel,
        out_shape=(jax.ShapeDtypeStruct((R_ALL, S5_WIDTH), F32),
                   jax.ShapeDtypeStruct((R_ALL, NA_WIDTH), BF16),
                   jax.ShapeDtypeStruct((R_ALL, NA_WIDTH), BF16),
                   jax.ShapeDtypeStruct((R_ALL, NA_WIDTH), BF16)),
        grid=(N_ALL_TILES,),
        in_specs=[
            pl.BlockSpec((TM, D_MODEL), lambda t: (t, 0)),
            pl.BlockSpec((1, N_MOD, D_MODEL), lambda t: (_mod_row(t), 0, 0)),
            _const_spec((D_MODEL, COL_Z)),
        ],
        out_specs=(row_spec, row_spec, row_spec, row_spec),
        compiler_params=_params(),
        name="mixer_in_proj",
    )(x, mod, w_ukvq)


def _s5_operators(lam_re, lam_im, log_dt, b_re, b_im, c_re, c_im):
    L, G, P, H = S5_CHUNK, S5_GROUPS, S5_STATE, S5_GROUP
    hi = lax.Precision.HIGHEST
    dt = jnp.exp(log_dt)[..., None]
    mag = jnp.exp(lam_re * dt)
    lr, li = mag * jnp.cos(lam_im * dt), mag * jnp.sin(lam_im * dt)
    den = lam_re * lam_re + lam_im * lam_im
    fr = ((lr - 1.0) * lam_re + li * lam_im) / den
    fi = (li * lam_re - (lr - 1.0) * lam_im) / den
    bb_re = fr[..., None] * b_re - fi[..., None] * b_im
    bb_im = fr[..., None] * b_im + fi[..., None] * b_re

    def powers(pr, pi, n):
        rs, is_ = [jnp.ones_like(pr)], [jnp.zeros_like(pi)]
        for _ in range(n):
            r, i = rs[-1], is_[-1]
            rs.append(r * pr - i * pi)
            is_.append(r * pi + i * pr)
        return jnp.stack(rs), jnp.stack(is_)

    pw_re, pw_im = powers(lr, li, L)
    cp_re, cp_im = powers(pw_re[L], pw_im[L], SCAN_ROWS)

    cl_re = c_re[None] * pw_re[:L, :, :, None, :] - c_im[None] * pw_im[:L, :, :, None, :]
    cl_im = c_re[None] * pw_im[:L, :, :, None, :] + c_im[None] * pw_re[:L, :, :, None, :]
    kk = (jnp.einsum('kdgap,dgph->kdgah', cl_re, bb_re, precision=hi)
          - jnp.einsum('kdgap,dgph->kdgah', cl_im, bb_im, precision=hi))
    kk = jnp.concatenate([kk, jnp.zeros_like(kk[:1])], 0)
    i_in = np.arange(L)[:, None]
    j_out = np.arange(L)[None, :]
    lag_f = np.where(j_out >= i_in, j_out - i_in, L)
    lag_b = np.where(i_in >= j_out, i_in - j_out, L)
    kin = kk[lag_f, 0] + kk[lag_b, 1]
    kin = jnp.transpose(kin, (2, 0, 4, 1, 3)).reshape(G, L * H, L * H)

    def local_state(d, exps):
        pr, pi = pw_re[exps, d], pw_im[exps, d]
        wr = pr[..., None] * bb_re[d][None] - pi[..., None] * bb_im[d][None]
        wi = pr[..., None] * bb_im[d][None] + pi[..., None] * bb_re[d][None]
        to = lambda w: jnp.transpose(w, (1, 0, 3, 2)).reshape(G, L * H, P)
        return to(wr), to(wi)

    wf_re, wf_im = local_state(0, np.arange(L - 1, -1, -1))
    wb_re, wb_im = local_state(1, np.arange(L))
    a_op = jnp.concatenate([kin, wf_re, wf_im, wb_re, wb_im], -1)

    def readout(d, exps):
        pr, pi = pw_re[exps, d][:, :, None, :], pw_im[exps, d][:, :, None, :]
        mr = c_re[d][None] * pr - c_im[d][None] * pi
        mi = -(c_re[d][None] * pi + c_im[d][None] * pr)
        to = lambda m: jnp.transpose(m, (1, 3, 0, 2)).reshape(G, P, L * H)
        return to(mr), to(mi)

    mf_re, mf_im = readout(0, np.arange(1, L + 1))
    mb_re, mb_im = readout(1, np.arange(L, 0, -1))
    o_op = jnp.concatenate([mf_re, mf_im, mb_re, mb_im], 1)

    def pair(r, i):
        return jnp.concatenate([r, r], -1), jnp.concatenate([-i, i], -1)

    tabs = []
    for d in range(2):
        exps = np.arange(1, SCAN_ROWS + 1) if d == 0 else np.arange(SCAN_ROWS, 0, -1)
        qa, qb = pair(jnp.transpose(cp_re[exps, d], (1, 0, 2)), jnp.transpose(cp_im[exps, d], (1, 0, 2)))
        steps = np.array([1, 2, 4])
        pa, pb = pair(jnp.transpose(cp_re[steps, d], (1, 0, 2)), jnp.transpose(cp_im[steps, d], (1, 0, 2)))
        pad = jnp.zeros((G, 2, 2 * P), F32)
        tabs.append(jnp.concatenate([qa, qb, pa, pb, pad], 1))
    return a_op.astype(BF16), o_op.astype(BF16), jnp.stack(tabs, 1)


def _s5_kernel(u_ref, a_ref, o_ref, tb_ref, y_ref, z_s, cin_s):
    P2 = 2 * S5_STATE
    z_s[...] = _dot(u_ref[0], a_ref[0])
    row = lax.broadcasted_iota(jnp.int32, (SCAN_ROWS, P2), 0)

    def cmul(a, b, x):
        return a * x + b * pltpu.roll(x, S5_STATE, 1)

    def tile_scan(d, x, carry):
        qa, qb = tb_ref[0, d, 0:8, :], tb_ref[0, d, 8:16, :]
        inc = x
        for j, s in enumerate((1, 2, 4)):
            pa, pb = tb_ref[0, d, 16 + j:17 + j, :], tb_ref[0, d, 19 + j:20 + j, :]
            if d == 0:
                sh = jnp.where(row >= s, pltpu.roll(inc, s, 0), 0.0)
            else:
                sh = jnp.where(row < SCAN_ROWS - s, pltpu.roll(inc, SCAN_ROWS - s, 0), 0.0)
            inc = inc + cmul(pa, pb, sh)
        inc = inc + cmul(qa, qb, carry)
        if d == 0:
            cin = jnp.where(row == 0, carry, pltpu.roll(inc, 1, 0))
            new = jnp.broadcast_to(inc[SCAN_ROWS - 1:SCAN_ROWS, :], (SCAN_ROWS, P2))
        else:
            cin = jnp.where(row == SCAN_ROWS - 1, carry, pltpu.roll(inc, SCAN_ROWS - 1, 0))
            new = jnp.broadcast_to(inc[0:1, :], (SCAN_ROWS, P2))
        return cin, new

    def step(d, r, carry):
        c0 = CHUNK_W + d * P2
        x = z_s[pl.ds(r, SCAN_ROWS), c0:c0 + P2]
        cin, new = tile_scan(d, x, carry)
        cin_s[pl.ds(r, SCAN_ROWS), d * P2:(d + 1) * P2] = cin
        return new

    zero = jnp.zeros((SCAN_ROWS, P2), F32)
    carries = [zero] * (2 * BATCH)
    ctx_tiles = CTX_CHUNKS // SCAN_ROWS
    lat_tiles = LAT_CHUNKS // SCAN_ROWS
    ctx0 = BATCH * LAT_CHUNKS
    for t in range(ctx_tiles):
        for b in range(BATCH):
            base = ctx0 + b * CTX_CHUNKS
            carries[b] = step(0, base + SCAN_ROWS * t, carries[b])
            carries[BATCH + b] = step(1, base + SCAN_ROWS * (ctx_tiles - 1 - t), carries[BATCH + b])

    def body(t, cs):
        cs = list(cs)
        for b in range(BATCH):
            rf = pl.multiple_of(b * LAT_CHUNKS + SCAN_ROWS * t, SCAN_ROWS)
            rb = pl.multiple_of(b * LAT_CHUNKS + SCAN_ROWS * (lat_tiles - 1 - t), SCAN_ROWS)
            cs[b] = step(0, rf, cs[b])
            cs[BATCH + b] = step(1, rb, cs[BATCH + b])
        return tuple(cs)

    lax.fori_loop(0, lat_tiles, body, tuple(carries))
    y_ref[0] = z_s[:, 0:CHUNK_W] + _dot(cin_s[...].astype(BF16), o_ref[0])


def _s5_mix(u, a_op, o_op, tabs):
    G, H, L = S5_GROUPS, S5_GROUP, S5_CHUNK
    ug = jnp.transpose(u.reshape(N_CHUNKS, L, G, H), (2, 0, 1, 3)).reshape(G, N_CHUNKS, CHUNK_W).astype(BF16)
    n_state = 4 * S5_STATE
    y = pl.pallas_call(
        _s5_kernel,
        out_shape=jax.ShapeDtypeStruct((G, N_CHUNKS, CHUNK_W), F32),
        grid=(G,),
        in_specs=[
            pl.BlockSpec((1, N_CHUNKS, CHUNK_W), lambda g: (g, 0, 0)),
            pl.BlockSpec((1, CHUNK_W, CHUNK_W + n_state), lambda g: (g, 0, 0)),
            pl.BlockSpec((1, n_state, CHUNK_W), lambda g: (g, 0, 0)),
            pl.BlockSpec((1, 2, 24, 2 * S5_STATE), lambda g: (g, 0, 0, 0)),
        ],
        out_specs=pl.BlockSpec((1, N_CHUNKS, CHUNK_W), lambda g: (g, 0, 0)),
        scratch_shapes=[pltpu.VMEM((N_CHUNKS, CHUNK_W + n_state), F32),
                        pltpu.VMEM((N_CHUNKS, n_state), F32)],
        compiler_params=_params(),
        name="s5_chunk_scan",
    )(ug, a_op, o_op, tabs)
    return jnp.transpose(y.reshape(G, N_CHUNKS, L, H), (1, 2, 0, 3)).reshape(R_ALL, S5_WIDTH)


def _att_window_start(i):
    return jnp.clip(i * ATT_QROWS - WIN_H // 2, 0, GRID_ROWS - ATT_KROWS)


def _att_class(i):
    return jnp.where(i == 0, 1, jnp.where(i == ATT_STEPS - 1, 2, 0))


def _att_bias(rpb):
    reps = np.array([1, 0, ATT_STEPS - 1])
    r = reps[:, None] * ATT_QROWS + np.arange(ATT_QROWS)[None, :]
    ws = np.clip(reps * ATT_QROWS - WIN_H // 2, 0, GRID_ROWS - ATT_KROWS)
    kr = ws[:, None] + np.arange(ATT_KROWS)[None, :]
    row_start = np.clip(r - WIN_H // 2, 0, GRID_ROWS - WIN_H)
    valid = (kr[:, None, :] >= row_start[:, :, None]) & (kr[:, None, :] < row_start[:, :, None] + WIN_H)
    dr = np.clip(kr[:, None, :] - r[:, :, None] + WIN_H - 1, 0, 2 * WIN_H - 2)
    col = np.arange(GRID_W)
    col_start = np.clip(col - WIN_W // 2, 0, GRID_W - WIN_W)
    col_mask = (col[None, :] >= col_start[:, None]) & (col[None, :] < col_start[:, None] + WIN_W)
    dc = np.clip(col[None, :] - col[:, None] + WIN_W - 1, 0, 2 * WIN_W - 2)
    bias = rpb.astype(F32)[:, dr][..., dc]
    mask = valid[None, :, :, :, None, None] & col_mask[None, None, None, None, :, :]
    bias = jnp.where(mask, bias, -jnp.inf)
    return jnp.transpose(bias, (1, 0, 2, 4, 3, 5)).reshape(3, NA_HEADS, ATT_TQ, ATT_TK)


def _attend_pair(q2, keys, vals, biases):
    lane = lax.broadcasted_iota(jnp.int32, q2.shape, 1)
    lo = lane < NA_HEAD_DIM
    outs = []
    for hh in range(2):
        qh = jnp.where(lo if hh == 0 else jnp.logical_not(lo), q2, jnp.zeros_like(q2))
        scores = []
        for kblk, bias in zip(keys, biases[hh]):
            s = _dot_t(qh, kblk)
            scores.append(s if bias is None else s + bias)
        m = functools.reduce(jnp.maximum, [s.max(-1, keepdims=True) for s in scores])
        ps = [jnp.exp(s - m) for s in scores]
        l = functools.reduce(jnp.add, [p.sum(-1, keepdims=True) for p in ps])
        o = functools.reduce(jnp.add, [_dot(p.astype(BF16), v) for p, v in zip(ps, vals)])
        outs.append(o / l)
    return jnp.where(lo, outs[0], outs[1])


def _att_kernel(q_ref, k_ref, v_ref, kc_ref, vc_ref, bias_ref, o_ref):
    i = pl.program_id(1)
    start = pl.multiple_of(_att_window_start(i) * GRID_W, GRID_W)
    for pr in range(NA_HEADS // 2):
        cs = slice(2 * NA_HEAD_DIM * pr, 2 * NA_HEAD_DIM * (pr + 1))
        keys = [k_ref[pl.ds(start, ATT_TK), cs], kc_ref[:, cs]]
        vals = [v_ref[pl.ds(start, ATT_TK), cs], vc_ref[:, cs]]
        biases = [[bias_ref[0, 2 * pr + hh], None] for hh in range(2)]
        o_ref[:, cs] = _attend_pair(q_ref[:, cs], keys, vals, biases).astype(BF16)


def _attention(q, k, v, bias):
    ctx_block = R_LAT // CTX_LEN
    return pl.pallas_call(
        _att_kernel,
        out_shape=jax.ShapeDtypeStruct((R_LAT, NA_WIDTH), BF16),
        grid=(BATCH, ATT_STEPS),
        in_specs=[
            pl.BlockSpec((ATT_TQ, NA_WIDTH), lambda b, i: (b * ATT_STEPS + i, 0)),
            pl.BlockSpec((SEQ, NA_WIDTH), lambda b, i: (b, 0)),
            pl.BlockSpec((SEQ, NA_WIDTH), lambda b, i: (b, 0)),
            pl.BlockSpec((CTX_LEN, NA_WIDTH), lambda b, i: (ctx_block + b, 0)),
            pl.BlockSpec((CTX_LEN, NA_WIDTH), lambda b, i: (ctx_block + b, 0)),
            pl.BlockSpec((1, NA_HEADS, ATT_TQ, ATT_TK), lambda b, i: (_att_class(i), 0, 0, 0)),
        ],
        out_specs=pl.BlockSpec((ATT_TQ, NA_WIDTH), lambda b, i: (b * ATT_STEPS + i, 0)),
        compiler_params=_params(2),
        name="neighbourhood_attention",
    )(q, k, v, k, v, bias)


def _ctx_att_kernel(q_ref, k_ref, v_ref, o_ref):
    for pr in range(NA_HEADS // 2):
        cs = slice(2 * NA_HEAD_DIM * pr, 2 * NA_HEAD_DIM * (pr + 1))
        o_ref[:, cs] = _attend_pair(q_ref[:, cs], [k_ref[:, cs]], [v_ref[:, cs]],
                                    [[None], [None]]).astype(BF16)


def _ctx_attention(q, k, v):
    ctx_block = R_LAT // CTX_LEN
    spec = pl.BlockSpec((CTX_LEN, NA_WIDTH), lambda b: (ctx_block + b, 0))
    return pl.pallas_call(
        _ctx_att_kernel,
        out_shape=jax.ShapeDtypeStruct((R_CTX, NA_WIDTH), BF16),
        grid=(BATCH,),
        in_specs=[spec, spec, spec],
        out_specs=pl.BlockSpec((CTX_LEN, NA_WIDTH), lambda b: (b, 0)),
        compiler_params=_params(),
        name="context_attention",
    )(q, k, v)


HALO = 8
SEGMENT_STARTS = (0, SEQ, R_LAT, R_LAT + CTX_LEN)
SEGMENT_ENDS = (SEQ - 1, R_LAT - 1, R_LAT + CTX_LEN - 1, R_ALL - 1)


def _merge_kernel(x_ref, xp_ref, xn_ref, ys_ref, u_ref, yb_ref, mod_ref, wzc_ref, wbg_ref, wglu_ref,
                  wbr_ref, wout_ref, d_ref, cw_ref, g_ref, b_ref, o_ref):
    t = pl.program_id(0)
    shift = mod_ref[0, 3:4, :]
    scale = mod_ref[0, 4:5, :]
    gate = mod_ref[0, 5:6, :]
    x = x_ref[...]
    xe = jnp.concatenate([xp_ref[...], x, xn_ref[...]], axis=0)
    he = (xe * (1.0 + scale) + shift).astype(BF16)

    zc = _dot(he, wzc_ref[...])
    vv = zc[:, 0:CONV_WIDTH] * zc[:, CONV_WIDTH:]
    rows = t * TM + lax.broadcasted_iota(jnp.int32, (TM, 1), 0)
    has_prev = functools.reduce(jnp.logical_and, [rows != s for s in SEGMENT_STARTS])
    has_next = functools.reduce(jnp.logical_and, [rows != e for e in SEGMENT_ENDS])
    cw = cw_ref[...]
    conv = (cw[0:1] * jnp.where(has_prev, vv[HALO - 1:HALO - 1 + TM], 0.0)
            + cw[1:2] * vv[HALO:HALO + TM]
            + cw[2:3] * jnp.where(has_next, vv[HALO + 1:HALO + 1 + TM], 0.0))
    bg = _dot(he[HALO:HALO + TM], wbg_ref[...])
    y_c = bg[:, 0:CONV_WIDTH] * conv

    ys = ys_ref[...] + d_ref[...] * u_ref[...]
    ge = jax.nn.gelu(ys)
    y_a = ge * jax.nn.sigmoid(_dot(ge.astype(BF16), wglu_ref[...]))

    gl = bg[:, CONV_WIDTH:]
    m = (jax.nn.sigmoid(gl[:, 0:D_MODEL]) * _dot(y_a.astype(BF16), wbr_ref[0])
         + jax.nn.sigmoid(gl[:, D_MODEL:2 * D_MODEL]) * _dot(yb_ref[...], wbr_ref[1])
         + jax.nn.sigmoid(gl[:, 2 * D_MODEL:]) * _dot(y_c.astype(BF16), wbr_ref[2]))
    mix = _dot(m.astype(BF16), wout_ref[...])
    o_ref[...] = _layer_norm(ALPHA * x + gate * mix, g_ref[...], b_ref[...])


def _merge(x, ys, u, yb, mod, wzc, wbg, wglu, wbr, wout, d, cw, g, b, n_tiles):
    halo_blocks = TM // HALO
    last_halo = R_ALL // HALO - 1
    row512 = pl.BlockSpec((TM, S5_WIDTH), lambda t: (t, 0))
    return pl.pallas_call(
        _merge_kernel,
        out_shape=jax.ShapeDtypeStruct((n_tiles * TM, D_MODEL), F32),
        grid=(n_tiles,),
        in_specs=[
            pl.BlockSpec((TM, D_MODEL), lambda t: (t, 0)),
            pl.BlockSpec((HALO, D_MODEL), lambda t: (jnp.maximum(t * halo_blocks - 1, 0), 0)),
            pl.BlockSpec((HALO, D_MODEL), lambda t: (jnp.minimum((t + 1) * halo_blocks, last_halo), 0)),
            row512, row512, row512,
            pl.BlockSpec((1, N_MOD, D_MODEL), lambda t: (_mod_row(t), 0, 0)),
            _const_spec((D_MODEL, 2 * CONV_WIDTH)),
            _const_spec((D_MODEL, CONV_WIDTH + 3 * D_MODEL)),
            _const_spec((S5_WIDTH, S5_WIDTH)),
            _const_spec((3, S5_WIDTH, D_MODEL)),
            _const_spec((D_MODEL, D_MODEL)),
            _const_spec((1, S5_WIDTH)),
            _const_spec((CONV_K, CONV_WIDTH)),
            _const_spec((1, D_MODEL)),
            _const_spec((1, D_MODEL)),
        ],
        out_specs=pl.BlockSpec((TM, D_MODEL), lambda t: (t, 0)),
        compiler_params=_params(),
        name="mixer_merge",
    )(x, x, x, ys, u, yb, mod, wzc, wbg, wglu, wbr, wout, d, cw, g, b)


def kernel(x, c, ctx, c_ctx, w_mod, b_mod, ln_g, ln_b, ffn_wg, ffn_wu, ffn_wd, w_in, s5_lam_re, s5_lam_im,
           s5_log_dt, s5_b_re, s5_b_im, s5_c_re, s5_c_im, s5_d, s5_w_glu, na_rpb, conv_w, w_branch, w_out):
    xa = jnp.concatenate([x.reshape(R_LAT, D_MODEL), ctx.reshape(R_CTX, D_MODEL)], 0)
    mods = _modulation(c, c_ctx, w_mod, b_mod)
    for l in range(DEPTH):
        last = l == DEPTH - 1
        mod = mods[l]
        ln = lambda i: (ln_g[l, i].reshape(1, D_MODEL), ln_b[l, i].reshape(1, D_MODEL))
        ffn_w = lambda i: (ffn_wg[l, i].astype(BF16), ffn_wu[l, i].astype(BF16), ffn_wd[l, i].astype(BF16))
        wi = w_in[l].astype(BF16)
        w_zc = jnp.concatenate([wi[:, COL_Z:COL_B], wi[:, COL_C:COL_G]], 1)
        w_bg = jnp.concatenate([wi[:, COL_B:COL_C], wi[:, COL_G:]], 1)

        xa = _ffn(xa, mod, *ffn_w(0), *ln(0), 0, N_ALL_TILES)
        u, k, v, q = _proj(xa, mod, wi[:, :COL_Z])
        ops = _s5_operators(s5_lam_re[l], s5_lam_im[l], s5_log_dt[l], s5_b_re[l], s5_b_im[l],
                            s5_c_re[l], s5_c_im[l])
        ys = _s5_mix(u, *ops)
        yb = _attention(q, k, v, _att_bias(na_rpb[l]))
        n_tiles = N_LAT_TILES if last else N_ALL_TILES
        if not last:
            yb = jnp.concatenate([yb, _ctx_attention(q, k, v)], 0)
        xa = _merge(xa, ys, u, yb, mod, w_zc, w_bg, s5_w_glu[l].astype(BF16), w_branch[l].astype(BF16),
                    w_out[l].astype(BF16), s5_d[l].reshape(1, S5_WIDTH), conv_w[l], *ln(1), n_tiles)
        xa = _ffn(xa, mod, *ffn_w(1), *ln(2), 6, n_tiles)
    return xa.reshape(BATCH, SEQ, D_MODEL)
```
